```python
import jax, jax.numpy as jnp
from jax import lax
import numpy as np

D_MODEL = 1024
BATCH = 4
SEQ = 4096
DEPTH = 4

HEAD_DIM = 64
N_HEADS = D_MODEL // HEAD_DIM
FOX_HEADS = N_HEADS // 2
MOBA_HEADS = N_HEADS - FOX_HEADS
FOX_WIDTH = FOX_HEADS * HEAD_DIM
MOBA_WIDTH = MOBA_HEADS * HEAD_DIM
D_FF = 2816
ROPE_THETA = 10000.0
FOX_Q_BLOCK = 128
MOBA_BLOCK = 256
MOBA_TOPK = 3
MOBA_Q_CHUNK = 64
NORM_EPS = 1e-6
N_MOD = 9
NEG_INF = -1e30
IN_SPLITS = [FOX_WIDTH, FOX_WIDTH, FOX_WIDTH, FOX_HEADS, MOBA_WIDTH, MOBA_WIDTH, MOBA_WIDTH]
IN_PROJ_WIDTH = sum(IN_SPLITS)
IN_SPLIT_POINTS = [int(v) for v in np.cumsum(IN_SPLITS)[:-1]]

kernel_name = "hybrid_fox_moba_macaron_adaln"


def rmsnorm(x, g):
    xf = x.astype(jnp.float32)
    y = xf * lax.rsqrt(jnp.mean(xf * xf, axis=-1, keepdims=True) + NORM_EPS)
    return (y * g.astype(jnp.float32)).astype(x.dtype)


def modulate(h, shift, scale):
    return h * (1.0 + scale[:, None, :]) + shift[:, None, :]


def swiglu(h, w_in, w_out):
    gate, up = jnp.split(h @ w_in, 2, axis=-1)
    return (jax.nn.silu(gate) * up) @ w_out


def to_heads(t, n_heads):
    b, s, _ = t.shape
    return t.reshape(b, s, n_heads, HEAD_DIM).transpose(0, 2, 1, 3)


def apply_rope(t, cos, sin):
    t1, t2 = jnp.split(t, 2, axis=-1)
    return jnp.concatenate([t1 * cos - t2 * sin, t2 * cos + t1 * sin], axis=-1).astype(t.dtype)


def fox_attention(q, k, v, logf):
    T = q.shape[2]
    scale = HEAD_DIM ** -0.5
    dcum = jnp.cumsum(logf, axis=-1)
    outs = []
    for start in range(0, T, FOX_Q_BLOCK):
        end = min(start + FOX_Q_BLOCK, T)
        qb, kb, vb = q[:, :, start:end], k[:, :, :end], v[:, :, :end]
        s = jnp.einsum('bhqd,bhkd->bhqk', qb, kb).astype(jnp.float32) * scale
        s = s + dcum[:, :, start:end, None] - dcum[:, :, None, :end]
        causal = (start + jnp.arange(end - start))[:, None] >= jnp.arange(end)[None, :]
        p = jax.nn.softmax(jnp.where(causal, s, NEG_INF), axis=-1)
        outs.append(jnp.einsum('bhqk,bhkd->bhqd', p.astype(v.dtype), vb))
    return jnp.concatenate(outs, axis=2)


def moba_attention(q, k, v):
    B, H, T, d = q.shape
    L = MOBA_BLOCK
    n_blk = -(-T // L)
    pad = n_blk * L - T
    kp = jnp.pad(k, ((0, 0), (0, 0), (0, pad), (0, 0)))
    vp = jnp.pad(v, ((0, 0), (0, 0), (0, pad), (0, 0)))
    kblk = kp.reshape(B, H, n_blk, L, d)
    vblk = vp.reshape(B, H, n_blk, L, d)
    kmean = jnp.mean(kblk.astype(jnp.float32), axis=3)
    gate = jnp.einsum('bhtd,bhnd->bhtn', q.astype(jnp.float32), kmean)
    q_blk = jnp.arange(T) // L
    past = jnp.arange(n_blk)[None, :] < q_blk[:, None]
    gate = jnp.where(past, gate, NEG_INF)
    k_sel = min(MOBA_TOPK, n_blk)
    _, top_idx = lax.top_k(gate, k_sel)
    sel_valid = jnp.arange(k_sel)[None, :] < q_blk[:, None]
    scale = HEAD_DIM ** -0.5
    bi = jnp.arange(B)[:, None, None, None]
    hi = jnp.arange(H)[None, :, None, None]
    outs = []
    for start in range(0, T, MOBA_Q_CHUNK):
        end = min(start + MOBA_Q_CHUNK, T)
        cn = end - start
        qc = q[:, :, start:end]
        idx = top_idx[:, :, start:end]
        ksel = kblk[bi, hi, idx]
        vsel = vblk[bi, hi, idx]
        s_sel = jnp.einsum('bhqd,bhqjld->bhqjl', qc, ksel).astype(jnp.float32) * scale
        s_sel = jnp.where(sel_valid[start:end, :, None], s_sel, NEG_INF)
        own = start // L
        k_own = kp[:, :, own * L:(own + 1) * L]
        v_own = vp[:, :, own * L:(own + 1) * L]
        s_own = jnp.einsum('bhqd,bhkd->bhqk', qc, k_own).astype(jnp.float32) * scale
        causal = (start + jnp.arange(cn))[:, None] >= (own * L + jnp.arange(L))[None, :]
        s_own = jnp.where(causal, s_own, NEG_INF)
        s = jnp.concatenate([s_sel.reshape(B, H, cn, k_sel * L), s_own], axis=-1)
        p = jax.nn.softmax(s, axis=-1).astype(v.dtype)
        p_sel = p[..., :k_sel * L].reshape(B, H, cn, k_sel, L)
        p_own = p[..., k_sel * L:]
        outs.append(jnp.einsum('bhqjl,bhqjld->bhqd', p_sel, vsel)
                    + jnp.einsum('bhqk,bhkd->bhqd', p_own, v_own))
    return jnp.concatenate(outs, axis=2)


def token_mixer(h, cos, sin, w_in, b_f, qk_g, w_out):
    B, T, _ = h.shape
    proj = h @ w_in
    fq, fk, fv, ff, mq, mk, mv = jnp.split(proj, IN_SPLIT_POINTS, axis=-1)
    fq = rmsnorm(to_heads(fq, FOX_HEADS), qk_g[0])
    fk = rmsnorm(to_heads(fk, FOX_HEADS), qk_g[1])
    fv = to_heads(fv, FOX_HEADS)
    logf = jax.nn.log_sigmoid(ff.astype(jnp.float32) + b_f.astype(jnp.float32)).transpose(0, 2, 1)
    o_fox = fox_attention(fq, fk, fv, logf)
    mq = apply_rope(rmsnorm(to_heads(mq, MOBA_HEADS), qk_g[2]), cos, sin)
    mk = apply_rope(rmsnorm(to_heads(mk, MOBA_HEADS), qk_g[3]), cos, sin)
    mv = to_heads(mv, MOBA_HEADS)
    o_moba = moba_attention(mq, mk, mv)
    o = jnp.concatenate([o_fox, o_moba], axis=1)
    o = o.transpose(0, 2, 1, 3).reshape(B, T, D_MODEL)
    return o @ w_out


def setup_inputs(seed: int = 0) -> dict:
    key = jax.random.key(seed)
    ks = jax.random.split(key, 12)
    f32 = jnp.float32
    x = jax.random.normal(ks[0], (BATCH, SEQ, D_MODEL), f32)
    c = jax.random.normal(ks[1], (BATCH, D_MODEL), f32)
    positions = jnp.broadcast_to(jnp.arange(SEQ, dtype=jnp.int32), (BATCH, SEQ))
    w_ada = jax.random.normal(ks[2], (DEPTH, D_MODEL, N_MOD * D_MODEL), f32) * (0.5 * D_MODEL ** -0.5)
    b_ada = jax.random.normal(ks[3], (DEPTH, N_MOD * D_MODEL), f32) * 0.01
    norm_g = 1.0 + 0.02 * jax.random.normal(ks[4], (DEPTH, 3, D_MODEL), f32)
    ffn_w_in = jax.random.normal(ks[5], (DEPTH, 2, D_MODEL, 2 * D_FF), f32) * D_MODEL ** -0.5
    ffn_w_out = jax.random.normal(ks[6], (DEPTH, 2, D_FF, D_MODEL), f32) * D_FF ** -0.5
    w_mix_in = jax.random.normal(ks[7], (DEPTH, D_MODEL, IN_PROJ_WIDTH), f32) * D_MODEL ** -0.5
    fox_b_f = 3.0 + 0.5 * jax.random.normal(ks[8], (DEPTH, FOX_HEADS), f32)
    qk_norm_g = 1.0 + 0.02 * jax.random.normal(ks[9], (DEPTH, 4, HEAD_DIM), f32)
    w_mix_out = jax.random.normal(ks[10], (DEPTH, D_MODEL, D_MODEL), f32) * D_MODEL ** -0.5
    return {"x": x, "c": c, "positions": positions, "w_ada": w_ada, "b_ada": b_ada,
            "norm_g": norm_g, "ffn_w_in": ffn_w_in, "ffn_w_out": ffn_w_out,
            "w_mix_in": w_mix_in, "fox_b_f": fox_b_f, "qk_norm_g": qk_norm_g,
            "w_mix_out": w_mix_out}


def reference(x, c, positions, w_ada, b_ada, norm_g, ffn_w_in, ffn_w_out,
              w_mix_in, fox_b_f, qk_norm_g, w_mix_out):
    inv_freq = ROPE_THETA ** (-jnp.arange(0, HEAD_DIM, 2, dtype=jnp.float32) / HEAD_DIM)
    ang = positions.astype(jnp.float32)[..., None] * inv_freq
    cos = jnp.cos(ang)[:, None].astype(x.dtype)
    sin = jnp.sin(ang)[:, None].astype(x.dtype)
    c_act = jax.nn.silu(c)
    for l in range(DEPTH):
        mod = c_act @ w_ada[l] + b_ada[l]
        sh1, sc1, g1, sh2, sc2, g2, sh3, sc3, g3 = jnp.split(mod, N_MOD, axis=-1)
        h = modulate(rmsnorm(x, norm_g[l, 0]), sh1, sc1)
        x = x + 0.5 * g1[:, None, :] * swiglu(h, ffn_w_in[l, 0], ffn_w_out[l, 0])
        h = modulate(rmsnorm(x, norm_g[l, 1]), sh2, sc2)
        x = x + g2[:, None, :] * token_mixer(h, cos, sin, w_mix_in[l], fox_b_f[l],
                                              qk_norm_g[l], w_mix_out[l])
        h = modulate(rmsnorm(x, norm_g[l, 2]), sh3, sc3)
        x = x + 0.5 * g3[:, None, :] * swiglu(h, ffn_w_in[l, 1], ffn_w_out[l, 1])
    return x
```

```python
import functools

import jax
import jax.numpy as jnp
from jax import lax
from jax.experimental import pallas as pl
from jax.experimental.pallas import tpu as pltpu

F32 = jnp.float32
BF16 = jnp.bfloat16

D_MODEL = 1024
HEAD_DIM = 64
FOX_HEADS = 8
MOBA_HEADS = 8
HALF_WIDTH = FOX_HEADS * HEAD_DIM
D_FF = 2816
ROPE_THETA = 10000.0
MOBA_BLOCK = 256
MOBA_TOPK = 3
NORM_EPS = 1e-6
N_MOD = 9
NEG_INF = -1e30
QK_SCALE = HEAD_DIM ** -0.5

LANES = 128
HEAD_PAIRS = HALF_WIDTH // LANES
MXU_DIM = 256
ATT_BLOCK = 256
FFN_CHUNKS = 2
TOKEN_TILE = 512
MIB = 1024 * 1024

_NT = (((1,), (1,)), ((), ()))


def _params(vmem_mib, n_axes):
    return pltpu.CompilerParams(dimension_semantics=("arbitrary",) * n_axes,
                                vmem_limit_bytes=vmem_mib * MIB)


def _resident(shape):
    return pl.BlockSpec(shape, lambda *_: (0,) * len(shape), pipeline_mode=pl.Buffered(1))


def _norm_mod(x, g, shift, scale):
    ms = jnp.mean(x * x, axis=-1, keepdims=True)
    return (x * lax.rsqrt(ms + NORM_EPS) * g) * (1.0 + scale) + shift


def _adaln_kernel(c_ref, w_ref, b_ref, o_ref):
    c = c_ref[...]
    act = c * (1.0 / (1.0 + jnp.exp(-c)))
    o_ref[0] = jnp.dot(act, w_ref[0], preferred_element_type=F32,
                       precision=lax.Precision.HIGHEST) + b_ref[0]


def _adaln(c, w_ada, b_ada):
    depth, d, width = w_ada.shape
    b = c.shape[0]
    rows = 8
    c_pad = jnp.pad(c, ((0, rows - b), (0, 0)))
    out = pl.pallas_call(
        _adaln_kernel,
        out_shape=jax.ShapeDtypeStruct((depth, rows, width), F32),
        grid=(depth, width // d),
        in_specs=[pl.BlockSpec((rows, d), lambda l, n: (0, 0)),
                  pl.BlockSpec((1, d, d), lambda l, n: (l, 0, n)),
                  pl.BlockSpec((1, 1, d), lambda l, n: (l, 0, n))],
        out_specs=pl.BlockSpec((1, rows, d), lambda l, n: (l, 0, n)),
        compiler_params=_params(32, 2),
        name="adaln_mod",
    )(c_pad, w_ada, b_ada.reshape(depth, 1, width))
    return out[:, :b].reshape(depth, b, N_MOD, d)


def _rope_kernel(pos_ref, freq_ref, cos_ref, sin_ref):
    ang = pos_ref[...].astype(F32) * freq_ref[...]
    lane = lax.broadcasted_iota(jnp.int32, ang.shape, 1)
    sin = jnp.sin(ang)
    cos_ref[...] = jnp.cos(ang)
    sin_ref[...] = jnp.where((lane & (HEAD_DIM - 1)) < HEAD_DIM // 2, -sin, sin)


def _rope_tables(positions):
    b, t = positions.shape
    n = b * t
    tm = min(TOKEN_TILE, n)
    inv_freq = ROPE_THETA ** (-jnp.arange(0, HEAD_DIM, 2, dtype=F32) / HEAD_DIM)
    freq = jnp.tile(inv_freq, LANES // (HEAD_DIM // 2)).reshape(1, LANES)
    cos, sin = pl.pallas_call(
        _rope_kernel,
        out_shape=(jax.ShapeDtypeStruct((n, LANES), F32),) * 2,
        grid=(n // tm,),
        in_specs=[pl.BlockSpec((tm, 1), lambda i: (i, 0)),
                  pl.BlockSpec((1, LANES), lambda i: (0, 0))],
        out_specs=(pl.BlockSpec((tm, LANES), lambda i: (i, 0)),) * 2,
        compiler_params=_params(32, 1),
        name="rope_tables",
    )(positions.reshape(n, 1), freq)
    return cos.reshape(b, t, LANES), sin.reshape(b, t, LANES)


def _ffn_kernel(x_ref, mod_ref, g_ref, win_ref, wout_ref, o_ref, *, k0):
    x = x_ref[0]
    mod = mod_ref[0]
    hb = _norm_mod(x, g_ref[...], mod[k0:k0 + 1], mod[k0 + 1:k0 + 2]).astype(BF16)
    tf = D_FF // FFN_CHUNKS
    acc = None
    for ch in range(FFN_CHUNKS):
        gate = jnp.dot(hb, win_ref[:, ch * tf:(ch + 1) * tf], preferred_element_type=F32)
        up = jnp.dot(hb, win_ref[:, D_FF + ch * tf:D_FF + (ch + 1) * tf],
                     preferred_element_type=F32)
        act = (gate * (1.0 / (1.0 + jnp.exp(-gate))) * up).astype(BF16)
        part = jnp.dot(act, wout_ref[ch * tf:(ch + 1) * tf, :], preferred_element_type=F32)
        acc = part if acc is None else acc + part
    o_ref[0] = x + (0.5 * mod[k0 + 2:k0 + 3]) * acc


def _ffn(x, mod_l, g, w_in, w_out, k0):
    b, t, d = x.shape
    tm = min(TOKEN_TILE, t)
    return pl.pallas_call(
        functools.partial(_ffn_kernel, k0=k0),
        out_shape=jax.ShapeDtypeStruct(x.shape, F32),
        grid=(b, t // tm),
        in_specs=[pl.BlockSpec((1, tm, d), lambda bi, ti: (bi, ti, 0)),
                  pl.BlockSpec((1, N_MOD, d), lambda bi, ti: (bi, 0, 0)),
                  pl.BlockSpec((1, d), lambda bi, ti: (0, 0)),
                  _resident((d, 2 * D_FF)),
                  _resident((D_FF, d))],
        out_specs=pl.BlockSpec((1, tm, d), lambda bi, ti: (bi, ti, 0)),
        compiler_params=_params(56, 2),
        name="swiglu_ffn",
    )(x, mod_l, g.reshape(1, d), w_in, w_out)


def _mixer_in_kernel(x_ref, mod_ref, g_ref, w_ref, gmat_ref, bf_ref, qkg_ref, cos_ref, sin_ref,
                     fq_ref, fk_ref, fv_ref, d_ref, mq_ref, mqf_ref, mk_ref, mv_ref, km_ref,
                     carry_ref, *, tm):
    ti = pl.program_id(1)
    x = x_ref[0]
    mod = mod_ref[0]
    hb = _norm_mod(x, g_ref[...], mod[3:4], mod[4:5]).astype(BF16)
    gmat = gmat_ref[...]
    qkg = qkg_ref[...]

    def proj(k):
        return jnp.dot(hb, w_ref[:, k * HALF_WIDTH:(k + 1) * HALF_WIDTH],
                       preferred_element_type=F32)

    def head_norm(tt, gain):
        sq = (tt * tt).astype(BF16)
        ms = jnp.concatenate(
            [jnp.dot(sq[:, k * MXU_DIM:(k + 1) * MXU_DIM], gmat, preferred_element_type=F32)
             for k in range(HALF_WIDTH // MXU_DIM)], axis=1)
        return tt * lax.rsqrt(ms + NORM_EPS) * gain

    lane = lax.broadcasted_iota(jnp.int32, (tm, LANES), 1)
    first_half = (lane & (HEAD_DIM - 1)) < HEAD_DIM // 2
    cosb = cos_ref[0]
    sinb = sin_ref[0]

    def rope(tt):
        outs = []
        for cb in range(HEAD_PAIRS):
            tc = tt[:, cb * LANES:(cb + 1) * LANES]
            partner = jnp.where(first_half,
                                pltpu.roll(tc, LANES - HEAD_DIM // 2, axis=1),
                                pltpu.roll(tc, HEAD_DIM // 2, axis=1))
            outs.append(tc * cosb + partner * sinb)
        return jnp.concatenate(outs, axis=1)

    fq_ref[0] = head_norm(proj(0), qkg[0:1]).astype(BF16)
    fk_ref[0] = head_norm(proj(1), qkg[1:2]).astype(BF16)
    fv_ref[0] = proj(2).astype(BF16)
    mq = rope(head_norm(proj(3), qkg[2:3]))
    mqf_ref[0] = mq
    mq_ref[0] = mq.astype(BF16)
    mk = rope(head_norm(proj(4), qkg[3:4]))
    mk_ref[0] = mk.astype(BF16)
    mv_ref[0] = proj(5).astype(BF16)
    for r in range(tm // MOBA_BLOCK):
        km_ref[0, 0, r:r + 1, :] = jnp.mean(mk[r * MOBA_BLOCK:(r + 1) * MOBA_BLOCK],
                                            axis=0, keepdims=True)

    z = jnp.dot(hb, w_ref[:, 6 * HALF_WIDTH:6 * HALF_WIDTH + LANES],
                preferred_element_type=F32) + bf_ref[...]
    cs = jnp.minimum(z, 0.0) - jnp.log1p(jnp.exp(-jnp.abs(z)))
    row = lax.broadcasted_iota(jnp.int32, (tm, LANES), 0)
    step = 1
    while step < tm:
        cs = cs + jnp.where(row >= step, pltpu.roll(cs, step, axis=0), 0.0)
        step *= 2

    @pl.when(ti == 0)
    def _():
        carry_ref[...] = jnp.zeros_like(carry_ref)

    cs = cs + carry_ref[...]
    carry_ref[...] = cs[tm - 1:tm, :]
    cs_t = cs.T
    for p in range(HEAD_PAIRS):
        for r in range(tm // ATT_BLOCK):
            d_ref[0, p, r] = cs_t[2 * p:2 * p + 2, r * ATT_BLOCK:(r + 1) * ATT_BLOCK]


def _mixer_in(x, mod_l, g, w, gmat, bf, qkg, cos_t, sin_t):
    b, t, d = x.shape
    tm = min(TOKEN_TILE, t)
    nt = t // tm
    hw = HALF_WIDTH
    tok = lambda bi, ti: (bi, ti, 0)
    act_spec = pl.BlockSpec((1, tm, hw), tok)
    bf16_act = jax.ShapeDtypeStruct((b, t, hw), BF16)
    d_shape = (b, HEAD_PAIRS, t // ATT_BLOCK, 2, ATT_BLOCK)
    km_shape = (b, nt, tm // MOBA_BLOCK, hw)
    outs = pl.pallas_call(
        functools.partial(_mixer_in_kernel, tm=tm),
        out_shape=(bf16_act, bf16_act, bf16_act,
                   jax.ShapeDtypeStruct(d_shape, F32),
                   bf16_act, jax.ShapeDtypeStruct((b, t, hw), F32), bf16_act, bf16_act,
                   jax.ShapeDtypeStruct(km_shape, F32)),
        grid=(b, nt),
        in_specs=[pl.BlockSpec((1, tm, d), tok),
                  pl.BlockSpec((1, N_MOD, d), lambda bi, ti: (bi, 0, 0)),
                  pl.BlockSpec((1, d), lambda bi, ti: (0, 0)),
                  _resident(w.shape),
                  _resident(gmat.shape),
                  pl.BlockSpec((1, LANES), lambda bi, ti: (0, 0)),
                  pl.BlockSpec((4, hw), lambda bi, ti: (0, 0)),
                  pl.BlockSpec((1, tm, LANES), tok),
                  pl.BlockSpec((1, tm, LANES), tok)],
        out_specs=(act_spec, act_spec, act_spec,
                   pl.BlockSpec((1, HEAD_PAIRS, tm // ATT_BLOCK, 2, ATT_BLOCK),
                                lambda bi, ti: (bi, 0, ti, 0, 0)),
                   act_spec, act_spec, act_spec, act_spec,
                   pl.BlockSpec((1, 1, tm // MOBA_BLOCK, hw), lambda bi, ti: (bi, ti, 0, 0))),
        scratch_shapes=[pltpu.VMEM((1, LANES), F32)],
        compiler_params=_params(56, 2),
        name="mixer_in",
    )(x, mod_l, g.reshape(1, d), w, gmat, bf, qkg, cos_t, sin_t)
    fq, fk, fv, dk, mq, mqf, mk, mv, km = outs
    return fq, fk, fv, dk, mq, mqf, mk, mv, km.reshape(b, t // MOBA_BLOCK, hw)


def _head_masks(rows):
    lane = lax.broadcasted_iota(jnp.int32, (rows, LANES), 1)
    return lane < HEAD_DIM


def _softmax_step(s, m, l):
    m_new = jnp.maximum(m, jnp.max(s, axis=-1, keepdims=True))
    p = jnp.exp(s - m_new)
    alpha = jnp.exp(m - m_new)
    return m_new, alpha * l + jnp.sum(p, axis=-1, keepdims=True), alpha, p


def _attn_init(tq):
    neg = jnp.full((tq, 1), NEG_INF, F32)
    zero = jnp.zeros((tq, 1), F32)
    return (neg, zero, neg, zero, jnp.zeros((tq, LANES), F32))


def _fox_kernel(q_ref, k_ref, v_ref, d_ref, o_ref, *, tq):
    i = pl.program_id(2)
    q = q_ref[0]
    is_a = _head_masks(tq)
    zero = jnp.zeros_like(q)
    q_heads = (jnp.where(is_a, q, zero), jnp.where(is_a, zero, q))
    causal = (lax.broadcasted_iota(jnp.int32, (tq, tq), 1)
              <= lax.broadcasted_iota(jnp.int32, (tq, tq), 0))

    def step(j, carry, diag):
        start = pl.multiple_of(j * tq, tq)
        kj = k_ref[0, pl.ds(start, tq), :]
        vj = v_ref[0, pl.ds(start, tq), :]
        dj = d_ref[0, 0, j]
        stats, alphas, pvs = [], [], []
        for h in range(2):
            s = lax.dot_general(q_heads[h], kj, _NT, preferred_element_type=F32)
            s = s - dj[h:h + 1, :]
            if diag:
                s = jnp.where(causal, s, NEG_INF)
            m_new, l_new, alpha, p = _softmax_step(s, carry[2 * h], carry[2 * h + 1])
            stats += [m_new, l_new]
            alphas.append(alpha)
            pvs.append(jnp.dot(p.astype(BF16), vj, preferred_element_type=F32))
        acc = (carry[4] * jnp.where(is_a, alphas[0], alphas[1])
               + jnp.where(is_a, pvs[0], pvs[1]))
        return (*stats, acc)

    carry = lax.fori_loop(0, i, lambda j, c: step(j, c, False), _attn_init(tq))
    carry = step(i, carry, True)
    o_ref[0] = (carry[4] / jnp.where(is_a, carry[1], carry[3])).astype(BF16)


def _fox(q, k, v, dk):
    b, t, hw = q.shape
    tq = ATT_BLOCK
    nk = t // tq
    return pl.pallas_call(
        functools.partial(_fox_kernel, tq=tq),
        out_shape=jax.ShapeDtypeStruct((b, t, hw), BF16),
        grid=(b, HEAD_PAIRS, nk),
        in_specs=[pl.BlockSpec((1, tq, LANES), lambda bi, p, i: (bi, i, p)),
                  pl.BlockSpec((1, t, LANES), lambda bi, p, i: (bi, 0, p)),
                  pl.BlockSpec((1, t, LANES), lambda bi, p, i: (bi, 0, p)),
                  pl.BlockSpec((1, 1, nk, 2, tq), lambda bi, p, i: (bi, p, 0, 0, 0))],
        out_specs=pl.BlockSpec((1, tq, LANES), lambda bi, p, i: (bi, i, p)),
        compiler_params=_params(32, 3),
        name="fox_attention",
    )(q, k, v, dk)


def _moba_kernel(q_ref, qf_ref, k_ref, v_ref, km_ref, o_ref, bias_ref, *, tq, nblk):
    i = pl.program_id(2)
    q = q_ref[0]
    qf = qf_ref[0]
    km = km_ref[0]
    is_a = _head_masks(tq)
    zero = jnp.zeros_like(q)
    q_heads = (jnp.where(is_a, q, zero), jnp.where(is_a, zero, q))
    qf_heads = (jnp.where(is_a, qf, 0.0), jnp.where(is_a, 0.0, qf))
    causal = (lax.broadcasted_iota(jnp.int32, (tq, tq), 1)
              <= lax.broadcasted_iota(jnp.int32, (tq, tq), 0))

    blk = lax.broadcasted_iota(jnp.int32, (nblk, tq), 0)
    for h in range(2):
        gate = lax.dot_general(km, qf_heads[h], _NT, preferred_element_type=F32,
                               precision=lax.Precision.HIGHEST)
        rank = jnp.zeros((nblk, tq), jnp.int32)
        for jp in range(nblk):
            gj = gate[jp:jp + 1, :]
            beats = (gj > gate) | ((gj == gate) & (jp < blk))
            rank = rank + jnp.where(beats, (jp < i).astype(jnp.int32), 0)
        keep = (blk < i) & (rank < MOBA_TOPK)
        bias_t = jnp.where(keep, 0.0, NEG_INF)
        bias_t = jnp.concatenate([bias_t, jnp.full((LANES - nblk, tq), NEG_INF, F32)], axis=0)
        bias = bias_t.T
        for j in range(nblk):
            bias_ref[h, j] = jnp.broadcast_to(bias[:, j:j + 1], (tq, LANES))

    def step(j, carry, diag):
        start = pl.multiple_of(j * tq, tq)
        kj = k_ref[0, pl.ds(start, tq), :]
        vj = v_ref[0, pl.ds(start, tq), :]
        stats, alphas, pvs = [], [], []
        for h in range(2):
            s = lax.dot_general(q_heads[h], kj, _NT, preferred_element_type=F32)
            if diag:
                s = jnp.where(causal, s, NEG_INF)
            else:
                bj = bias_ref[h, j]
                s = s + jnp.concatenate([bj] * (tq // LANES), axis=1)
            m_new, l_new, alpha, p = _softmax_step(s, carry[2 * h], carry[2 * h + 1])
            stats += [m_new, l_new]
            alphas.append(alpha)
            pvs.append(jnp.dot(p.astype(BF16), vj, preferred_element_type=F32))
        acc = (carry[4] * jnp.where(is_a, alphas[0], alphas[1])
               + jnp.where(is_a, pvs[0], pvs[1]))
        return (*stats, acc)

    carry = step(i, _attn_init(tq), True)
    carry = lax.fori_loop(0, i, lambda j, c: step(j, c, False), carry)
    o_ref[0] = (carry[4] / jnp.where(is_a, carry[1], carry[3])).astype(BF16)


def _moba(q, qf, k, v, km):
    b, t, hw = q.shape
    tq = ATT_BLOCK
    assert tq == MOBA_BLOCK
    nblk = t // MOBA_BLOCK
    return pl.pallas_call(
        functools.partial(_moba_kernel, tq=tq, nblk=nblk),
        out_shape=jax.ShapeDtypeStruct((b, t, hw), BF16),
        grid=(b, HEAD_PAIRS, nblk),
        in_specs=[pl.BlockSpec((1, tq, LANES), lambda bi, p, i: (bi, i, p)),
                  pl.BlockSpec((1, tq, LANES), lambda bi, p, i: (bi, i, p)),
                  pl.BlockSpec((1, t, LANES), lambda bi, p, i: (bi, 0, p)),
                  pl.BlockSpec((1, t, LANES), lambda bi, p, i: (bi, 0, p)),
                  pl.BlockSpec((1, nblk, LANES), lambda bi, p, i: (bi, 0, p))],
        out_specs=pl.BlockSpec((1, tq, LANES), lambda bi, p, i: (bi, i, p)),
        scratch_shapes=[pltpu.VMEM((2, nblk, tq, LANES), F32)],
        compiler_params=_params(32, 3),
        name="moba_attention",
    )(q, qf, k, v, km)


def _mixer_out_kernel(x_ref, of_ref, om_ref, mod_ref, wf_ref, wm_ref, o_ref):
    y = (jnp.dot(of_ref[0], wf_ref[...], preferred_element_type=F32)
         + jnp.dot(om_ref[0], wm_ref[...], preferred_element_type=F32))
    o_ref[0] = x_ref[0] + mod_ref[0][5:6] * y


def _mixer_out(x, o_fox, o_moba, mod_l, w_fox, w_moba):
    b, t, d = x.shape
    tm = min(TOKEN_TILE, t)
    tok = lambda bi, ti: (bi, ti, 0)
    return pl.pallas_call(
        _mixer_out_kernel,
        out_shape=jax.ShapeDtypeStruct(x.shape, F32),
        grid=(b, t // tm),
        in_specs=[pl.BlockSpec((1, tm, d), tok),
                  pl.BlockSpec((1, tm, HALF_WIDTH), tok),
                  pl.BlockSpec((1, tm, HALF_WIDTH), tok),
                  pl.BlockSpec((1, N_MOD, d), lambda bi, ti: (bi, 0, 0)),
                  _resident(w_fox.shape),
                  _resident(w_moba.shape)],
        out_specs=pl.BlockSpec((1, tm, d), tok),
        compiler_params=_params(32, 2),
        name="mixer_out",
    )(x, o_fox, o_moba, mod_l, w_fox, w_moba)


def kernel(x, c, positions, w_ada, b_ada, norm_g, ffn_w_in, ffn_w_out, w_mix_in, fox_b_f,
           qk_norm_g, w_mix_out):
    b, t, d = x.shape
    depth = w_ada.shape[0]
    assert d == D_MODEL and t % TOKEN_TILE == 0 and TOKEN_TILE % MOBA_BLOCK == 0
    hw = HALF_WIDTH

    mod = _adaln(c, w_ada, b_ada)
    cos_t, sin_t = _rope_tables(positions)

    w_in = ffn_w_in.astype(BF16)
    w_out = ffn_w_out.astype(BF16)
    f_lo = 3 * hw
    m_lo = f_lo + FOX_HEADS
    w_gate = jnp.pad(w_mix_in[:, :, f_lo:m_lo], ((0, 0), (0, 0), (0, LANES - FOX_HEADS)))
    w_mix = jnp.concatenate([w_mix_in[:, :, :f_lo], w_mix_in[:, :, m_lo:], w_gate],
                            axis=-1).astype(BF16)
    w_o = w_mix_out.astype(BF16)
    bf = jnp.pad(fox_b_f, ((0, 0), (0, LANES - FOX_HEADS))).reshape(depth, 1, LANES)
    gains = jnp.tile(qk_norm_g, (1, 1, FOX_HEADS))
    gains = gains * jnp.array([QK_SCALE, 1.0, QK_SCALE, 1.0], F32)[None, :, None]
    heads_per_tile = MXU_DIM // HEAD_DIM
    gmat = jnp.kron(jnp.eye(heads_per_tile, dtype=F32),
                    jnp.full((HEAD_DIM, HEAD_DIM), 1.0 / HEAD_DIM, F32)).astype(BF16)

    for l in range(depth):
        x = _ffn(x, mod[l], norm_g[l, 0], w_in[l, 0], w_out[l, 0], k0=0)
        fq, fk, fv, dk, mq, mqf, mk, mv, km = _mixer_in(
            x, mod[l], norm_g[l, 1], w_mix[l], gmat, bf[l], gains[l], cos_t, sin_t)
        o_fox = _fox(fq, fk, fv, dk)
        o_moba = _moba(mq, mqf, mk, mv, km)
        x = _mixer_out(x, o_fox, o_moba, mod[l], w_o[l, :hw], w_o[l, hw:])
        x = _ffn(x, mod[l], norm_g[l, 2], w_in[l, 1], w_out[l, 1], k0=6)
    return x
```

```python
import functools
import math

import jax
import jax.numpy as jnp
from jax import lax
from jax.experimental import pallas as pl
from jax.experimental.pallas import tpu as pltpu

F32 = jnp.float32
BF16 = jnp.bfloat16

D_MODEL = 1024
HEAD_DIM = 64
FOX_HEADS = 8
MOBA_HEADS = 8
HALF_WIDTH = FOX_HEADS * HEAD_DIM
D_FF = 2816
ROPE_THETA = 10000.0
MOBA_BLOCK = 256
MOBA_TOPK = 3
NORM_EPS = 1e-6
N_MOD = 9
NEG_INF = -1e30
LOG2E = math.log2(math.e)
QK_SCALE = HEAD_DIM ** -0.5

LANES = 128
HEAD_PAIRS = HALF_WIDTH // LANES
MXU_DIM = 256
K_AUG = MXU_DIM
AUG_ROWS = 16
D_TERMS = 3
ATT_TILE = 512
PAIRS_PER_STEP = 2
FFN_CHUNKS = 2
TOKEN_TILE = 512
MIB = 1024 * 1024


def _params(vmem_mib, n_axes):
    return pltpu.CompilerParams(dimension_semantics=("arbitrary",) * n_axes,
                                vmem_limit_bytes=vmem_mib * MIB)


def _resident(shape):
    return pl.BlockSpec(shape, lambda *_: (0,) * len(shape), pipeline_mode=pl.Buffered(1))


def _norm_mod(x, g, shift, scale):
    ms = jnp.mean(x * x, axis=-1, keepdims=True)
    return (x * lax.rsqrt(ms + NORM_EPS) * g) * (1.0 + scale) + shift


def _adaln_kernel(c_ref, w_ref, b_ref, o_ref):
    c = c_ref[...]
    act = c * (1.0 / (1.0 + jnp.exp(-c)))
    o_ref[0] = jnp.dot(act, w_ref[0], preferred_element_type=F32,
                       precision=lax.Precision.HIGHEST) + b_ref[0]


def _adaln(c, w_ada, b_ada):
    depth, d, width = w_ada.shape
    b = c.shape[0]
    rows = 8
    c_pad = jnp.pad(c, ((0, rows - b), (0, 0)))
    out = pl.pallas_call(
        _adaln_kernel,
        out_shape=jax.ShapeDtypeStruct((depth, rows, width), F32),
        grid=(depth, width // d),
        in_specs=[pl.BlockSpec((rows, d), lambda l, n: (0, 0)),
                  pl.BlockSpec((1, d, d), lambda l, n: (l, 0, n)),
                  pl.BlockSpec((1, 1, d), lambda l, n: (l, 0, n))],
        out_specs=pl.BlockSpec((1, rows, d), lambda l, n: (l, 0, n)),
        compiler_params=_params(32, 2),
        name="adaln_mod",
    )(c_pad, w_ada, b_ada.reshape(depth, 1, width))
    return out[:, :b].reshape(depth, b, N_MOD, d)


def _rope_kernel(pos_ref, freq_ref, cos_ref, sin_ref):
    ang = pos_ref[...].astype(F32) * freq_ref[...]
    lane = lax.broadcasted_iota(jnp.int32, ang.shape, 1)
    sin = jnp.sin(ang)
    cos_ref[...] = jnp.cos(ang)
    sin_ref[...] = jnp.where((lane & (HEAD_DIM - 1)) < HEAD_DIM // 2, -sin, sin)


def _rope_tables(positions):
    b, t = positions.shape
    n = b * t
    tm = min(TOKEN_TILE, n)
    inv_freq = ROPE_THETA ** (-jnp.arange(0, HEAD_DIM, 2, dtype=F32) / HEAD_DIM)
    freq = jnp.tile(inv_freq, LANES // (HEAD_DIM // 2)).reshape(1, LANES)
    cos, sin = pl.pallas_call(
        _rope_kernel,
        out_shape=(jax.ShapeDtypeStruct((n, LANES), F32),) * 2,
        grid=(n // tm,),
        in_specs=[pl.BlockSpec((tm, 1), lambda i: (i, 0)),
                  pl.BlockSpec((1, LANES), lambda i: (0, 0))],
        out_specs=(pl.BlockSpec((tm, LANES), lambda i: (i, 0)),) * 2,
        compiler_params=_params(32, 1),
        name="rope_tables",
    )(positions.reshape(n, 1), freq)
    return cos.reshape(b, t, LANES), sin.reshape(b, t, LANES)


def _ffn_kernel(x_ref, mod_ref, g_ref, win_ref, wout_ref, o_ref, *, k0):
    x = x_ref[0]
    mod = mod_ref[0]
    hb = _norm_mod(x, g_ref[...], mod[k0:k0 + 1], mod[k0 + 1:k0 + 2]).astype(BF16)
    tf = D_FF // FFN_CHUNKS
    acc = None
    for ch in range(FFN_CHUNKS):
        gate = jnp.dot(hb, win_ref[:, ch * tf:(ch + 1) * tf], preferred_element_type=F32)
        up = jnp.dot(hb, win_ref[:, D_FF + ch * tf:D_FF + (ch + 1) * tf],
                     preferred_element_type=F32)
        act = (gate * (1.0 / (1.0 + jnp.exp(-gate))) * up).astype(BF16)
        part = jnp.dot(act, wout_ref[ch * tf:(ch + 1) * tf, :], preferred_element_type=F32)
        acc = part if acc is None else acc + part
    o_ref[0] = x + (0.5 * mod[k0 + 2:k0 + 3]) * acc


def _ffn(x, mod_l, g, w_in, w_out, k0):
    b, t, d = x.shape
    tm = min(TOKEN_TILE, t)
    return pl.pallas_call(
        functools.partial(_ffn_kernel, k0=k0),
        out_shape=jax.ShapeDtypeStruct(x.shape, F32),
        grid=(b, t // tm),
        in_specs=[pl.BlockSpec((1, tm, d), lambda bi, ti: (bi, ti, 0)),
                  pl.BlockSpec((1, N_MOD, d), lambda bi, ti: (bi, 0, 0)),
                  pl.BlockSpec((1, d), lambda bi, ti: (0, 0)),
                  _resident((d, 2 * D_FF)),
                  _resident((D_FF, d))],
        out_specs=pl.BlockSpec((1, tm, d), lambda bi, ti: (bi, ti, 0)),
        compiler_params=_params(56, 2),
        name="swiglu_ffn",
    )(x, mod_l, g.reshape(1, d), w_in, w_out)


def _mixer_in_kernel(x_ref, mod_ref, g_ref, w_ref, gmat_ref, bf_ref, qkg_ref, cos_ref, sin_ref,
                     sel_ref,
                     fqt_ref, fk_ref, fvt_ref, mqt_ref, mqft_ref, mk_ref, mvt_ref, km_ref,
                     carry_ref, *, tm):
    ti = pl.program_id(1)
    x = x_ref[0]
    mod = mod_ref[0]
    hb = _norm_mod(x, g_ref[...], mod[3:4], mod[4:5]).astype(BF16)
    gmat = gmat_ref[...]
    qkg = qkg_ref[...]

    def proj(k):
        return jnp.dot(hb, w_ref[:, k * HALF_WIDTH:(k + 1) * HALF_WIDTH],
                       preferred_element_type=F32)

    def head_norm(tt, gain):
        sq = (tt * tt).astype(BF16)
        ms = jnp.concatenate(
            [jnp.dot(sq[:, k * MXU_DIM:(k + 1) * MXU_DIM], gmat, preferred_element_type=F32)
             for k in range(HALF_WIDTH // MXU_DIM)], axis=1)
        return tt * lax.rsqrt(ms + NORM_EPS) * gain

    lane = lax.broadcasted_iota(jnp.int32, (tm, LANES), 1)
    row = lax.broadcasted_iota(jnp.int32, (tm, LANES), 0)
    first_half = (lane & (HEAD_DIM - 1)) < HEAD_DIM // 2
    cosb = cos_ref[0]
    sinb = sin_ref[0]

    def rope(tt):
        outs = []
        for cb in range(HEAD_PAIRS):
            tc = tt[:, cb * LANES:(cb + 1) * LANES]
            partner = jnp.where(first_half,
                                pltpu.roll(tc, LANES - HEAD_DIM // 2, axis=1),
                                pltpu.roll(tc, HEAD_DIM // 2, axis=1))
            outs.append(tc * cosb + partner * sinb)
        return jnp.concatenate(outs, axis=1)

    def store_keys(k_ref, keys, aug):
        kb = keys.astype(BF16)
        for p in range(HEAD_PAIRS):
            k_ref[0, p, :, 0:LANES] = kb[:, p * LANES:(p + 1) * LANES]
            k_ref[0, p, :, LANES:K_AUG] = aug[p]

    def store_values_t(vt_ref, vals):
        vt = vals.T.astype(BF16)
        for p in range(HEAD_PAIRS):
            for r in range(tm // ATT_TILE):
                vt_ref[0, p, r] = vt[p * LANES:(p + 1) * LANES, r * ATT_TILE:(r + 1) * ATT_TILE]

    z = jnp.dot(hb, w_ref[:, 6 * HALF_WIDTH:6 * HALF_WIDTH + LANES],
                preferred_element_type=F32) + bf_ref[...]
    cs = jnp.minimum(z, 0.0) - jnp.log1p(jnp.exp(-jnp.abs(z)))
    step = 1
    while step < tm:
        cs = cs + jnp.where(row >= step, pltpu.roll(cs, step, axis=0), 0.0)
        step *= 2

    @pl.when(ti == 0)
    def _():
        carry_ref[...] = jnp.zeros_like(carry_ref)

    cs = cs + carry_ref[...]
    carry_ref[...] = cs[tm - 1:tm, :]
    rest = cs * (-LOG2E)
    terms = []
    for _ in range(D_TERMS):
        terms.append(rest.astype(BF16))
        rest = rest - terms[-1].astype(F32)
    aug_fb = jnp.dot(jnp.concatenate(terms, axis=1), sel_ref[...],
                     preferred_element_type=F32).astype(BF16)

    fq = head_norm(proj(0), qkg[0:1])
    fqt_ref[0] = fq.T.astype(BF16)
    store_keys(fk_ref, head_norm(proj(1), qkg[1:2]),
               [aug_fb[:, p * LANES:(p + 1) * LANES] for p in range(HEAD_PAIRS)])
    store_values_t(fvt_ref, proj(2))

    mq_t = rope(head_norm(proj(3), qkg[2:3])).T
    mqft_ref[0] = mq_t
    mqt_ref[0] = (mq_t * (QK_SCALE * LOG2E)).astype(BF16)
    mk = rope(head_norm(proj(4), qkg[3:4]))
    block_onehot = jnp.where(lane == ((ti * tm + row) // MOBA_BLOCK), 1.0, 0.0).astype(BF16)
    store_keys(mk_ref, mk, [block_onehot] * HEAD_PAIRS)
    store_values_t(mvt_ref, proj(5))
    for r in range(tm // MOBA_BLOCK):
        km_ref[0, 0, r:r + 1, :] = jnp.mean(mk[r * MOBA_BLOCK:(r + 1) * MOBA_BLOCK],
                                            axis=0, keepdims=True)


def _mixer_in(x, mod_l, g, w, gmat, bf, qkg, cos_t, sin_t, sel):
    b, t, d = x.shape
    tm = min(TOKEN_TILE, t)
    nt = t // tm
    hw = HALF_WIDTH
    tok = lambda bi, ti: (bi, ti, 0)
    qt_spec = pl.BlockSpec((1, hw, tm), lambda bi, ti: (bi, 0, ti))
    k_spec = pl.BlockSpec((1, HEAD_PAIRS, tm, K_AUG), lambda bi, ti: (bi, 0, ti, 0))
    vt_spec = pl.BlockSpec((1, HEAD_PAIRS, tm // ATT_TILE, LANES, ATT_TILE),
                           lambda bi, ti: (bi, 0, ti, 0, 0))
    qt_shape = jax.ShapeDtypeStruct((b, hw, t), BF16)
    k_shape = jax.ShapeDtypeStruct((b, HEAD_PAIRS, t, K_AUG), BF16)
    vt_shape = jax.ShapeDtypeStruct((b, HEAD_PAIRS, t // ATT_TILE, LANES, ATT_TILE), BF16)
    outs = pl.pallas_call(
        functools.partial(_mixer_in_kernel, tm=tm),
        out_shape=(qt_shape, k_shape, vt_shape,
                   qt_shape, jax.ShapeDtypeStruct((b, hw, t), F32), k_shape, vt_shape,
                   jax.ShapeDtypeStruct((b, nt, tm // MOBA_BLOCK, hw), F32)),
        grid=(b, nt),
        in_specs=[pl.BlockSpec((1, tm, d), tok),
                  pl.BlockSpec((1, N_MOD, d), lambda bi, ti: (bi, 0, 0)),
                  pl.BlockSpec((1, d), lambda bi, ti: (0, 0)),
                  _resident(w.shape),
                  _resident(gmat.shape),
                  pl.BlockSpec((1, LANES), lambda bi, ti: (0, 0)),
                  pl.BlockSpec((4, hw), lambda bi, ti: (0, 0)),
                  pl.BlockSpec((1, tm, LANES), tok),
                  pl.BlockSpec((1, tm, LANES), tok),
                  _resident(sel.shape)],
        out_specs=(qt_spec, k_spec, vt_spec, qt_spec, qt_spec, k_spec, vt_spec,
                   pl.BlockSpec((1, 1, tm // MOBA_BLOCK, hw), lambda bi, ti: (bi, ti, 0, 0))),
        scratch_shapes=[pltpu.VMEM((1, LANES), F32)],
        compiler_params=_params(56, 2),
        name="mixer_in",
    )(x, mod_l, g.reshape(1, d), w, gmat, bf, qkg, cos_t, sin_t, sel)
    fqt, fk, fvt, mqt, mqft, mk, mvt, km = outs
    return fqt, fk, fvt, mqt, mqft, mk, mvt, km.reshape(b, t // MOBA_BLOCK, hw)


def _moba_bias(qft, km, i, tq, nblk):
    gate = jnp.dot(km, qft, preferred_element_type=F32,
                   precision=lax.Precision.HIGHEST)
    blk = lax.broadcasted_iota(jnp.int32, (nblk, tq), 0)
    qblk = (i * tq + lax.broadcasted_iota(jnp.int32, (1, tq), 1)) // MOBA_BLOCK
    rank = jnp.zeros((nblk, tq), jnp.int32)
    for jp in range(nblk):
        gj = gate[jp:jp + 1, :]
        beats = (gj > gate) | ((gj == gate) & (jp < blk))
        rank = rank + jnp.where(beats & (jp < qblk), 1, 0)
    keep = ((blk < qblk) & (rank < MOBA_TOPK)) | (blk == qblk)
    bias = jnp.where(keep, 0.0, NEG_INF)
    if nblk < AUG_ROWS:
        bias = jnp.concatenate([bias, jnp.full((AUG_ROWS - nblk, tq), NEG_INF, F32)], axis=0)
    return bias


def _attn_kernel(*refs, tq, npairs, nblk, moba):
    if moba:
        qt_ref, qft_ref, km_ref, k_ref, vt_ref, o_ref, qaug_ref, acc_ref, m_ref, l_ref = refs
    else:
        qt_ref, k_ref, vt_ref, o_ref, qaug_ref, acc_ref, m_ref, l_ref = refs
    i = pl.program_id(2)
    nheads = 2 * npairs
    half = jnp.zeros((HEAD_DIM, tq), BF16)
    tail = jnp.zeros((K_AUG - LANES - AUG_ROWS, tq), BF16)
    aug_row = lax.broadcasted_iota(jnp.int32, (AUG_ROWS, tq), 0)

    for pp in range(npairs):
        qt = qt_ref[0, pp * LANES:(pp + 1) * LANES, :]
        for h in range(2):
            if moba:
                qft = qft_ref[0, pp * LANES:(pp + 1) * LANES, :]
                zf = jnp.zeros((HEAD_DIM, tq), F32)
                qft_h = (jnp.concatenate([qft[:HEAD_DIM], zf], axis=0) if h == 0
                         else jnp.concatenate([zf, qft[HEAD_DIM:]], axis=0))
                aug = _moba_bias(qft_h, km_ref[0, :, pp * LANES:(pp + 1) * LANES], i, tq, nblk)
            else:
                aug = jnp.where((aug_row >= D_TERMS * h) & (aug_row < D_TERMS * (h + 1)), 1.0, 0.0)
            top = (jnp.concatenate([qt[:HEAD_DIM], half], axis=0) if h == 0
                   else jnp.concatenate([half, qt[HEAD_DIM:]], axis=0))
            qaug_ref[2 * pp + h] = jnp.concatenate([top, aug.astype(BF16), tail], axis=0)

    m_ref[...] = jnp.full(m_ref.shape, NEG_INF, F32)
    l_ref[...] = jnp.zeros(l_ref.shape, F32)
    acc_ref[...] = jnp.zeros(acc_ref.shape, F32)
    visible = (lax.broadcasted_iota(jnp.int32, (tq, tq), 0)
               <= lax.broadcasted_iota(jnp.int32, (tq, tq), 1))

    def step(j, diag):
        start = pl.multiple_of(j * tq, tq)
        for pp in range(npairs):
            kj = k_ref[0, pp, pl.ds(start, tq), :]
            vtj = vt_ref[0, pp, j]
            for h in range(2):
                hh = 2 * pp + h
                s = jnp.dot(kj, qaug_ref[hh], preferred_element_type=F32)
                if diag:
                    s = jnp.where(visible, s, NEG_INF)
                m_old = m_ref[hh]
                m_new = jnp.maximum(m_old, jnp.max(s, axis=0, keepdims=True))
                p = jnp.exp2(s - m_new)
                alpha = jnp.exp2(m_old - m_new)
                l_ref[hh] = alpha * l_ref[hh] + jnp.sum(p, axis=0, keepdims=True)
                m_ref[hh] = m_new
                pv = jnp.dot(vtj, p.astype(BF16), preferred_element_type=F32)
                lo, hi = h * HEAD_DIM, (h + 1) * HEAD_DIM
                acc_ref[pp, lo:hi, :] = alpha * acc_ref[pp, lo:hi, :] + pv[lo:hi, :]

    step(i, True)

    def body(j, carry):
        step(j, False)
        return carry

    lax.fori_loop(0, i, body, 0)

    for pp in range(npairs):
        denom = jnp.concatenate([jnp.broadcast_to(l_ref[2 * pp + h], (HEAD_DIM, tq))
                                 for h in range(2)], axis=0)
        o_ref[0, :, pp * LANES:(pp + 1) * LANES] = (acc_ref[pp] / denom).T.astype(BF16)


def _attention(qt, k_aug, vt, qft=None, km=None):
    b, hw, t = qt.shape
    tq = min(ATT_TILE, t)
    npairs = PAIRS_PER_STEP
    width = npairs * LANES
    nblk = t // MOBA_BLOCK
    moba = qft is not None
    assert t % tq == 0 and nblk <= AUG_ROWS and vt.shape[-1] == tq
    q_spec = pl.BlockSpec((1, width, tq), lambda bi, g, i: (bi, g, i))
    in_specs = [q_spec]
    args = [qt]
    if moba:
        in_specs += [q_spec, pl.BlockSpec((1, nblk, width), lambda bi, g, i: (bi, 0, g))]
        args += [qft, km]
    in_specs += [pl.BlockSpec((1, npairs, t, K_AUG), lambda bi, g, i: (bi, g, 0, 0)),
                 pl.BlockSpec((1, npairs, t // tq, LANES, tq), lambda bi, g, i: (bi, g, 0, 0, 0))]
    args += [k_aug, vt]
    return pl.pallas_call(
        functools.partial(_attn_kernel, tq=tq, npairs=npairs, nblk=nblk, moba=moba),
        out_shape=jax.ShapeDtypeStruct((b, t, hw), BF16),
        grid=(b, HEAD_PAIRS // npairs, t // tq),
        in_specs=in_specs,
        out_specs=pl.BlockSpec((1, tq, width), lambda bi, g, i: (bi, i, g)),
        scratch_shapes=[pltpu.VMEM((2 * npairs, K_AUG, tq), BF16),
                        pltpu.VMEM((npairs, LANES, tq), F32),
                        pltpu.VMEM((2 * npairs, 1, tq), F32),
                        pltpu.VMEM((2 * npairs, 1, tq), F32)],
        compiler_params=_params(48, 3),
        name="moba_attention" if moba else "fox_attention",
    )(*args)


def _mixer_out_kernel(x_ref, of_ref, om_ref, mod_ref, wf_ref, wm_ref, o_ref):
    y = (jnp.dot(of_ref[0], wf_ref[...], preferred_element_type=F32)
         + jnp.dot(om_ref[0], wm_ref[...], preferred_element_type=F32))
    o_ref[0] = x_ref[0] + mod_ref[0][5:6] * y


def _mixer_out(x, o_fox, o_moba, mod_l, w_fox, w_moba):
    b, t, d = x.shape
    tm = min(TOKEN_TILE, t)
    tok = lambda bi, ti: (bi, ti, 0)
    return pl.pallas_call(
        _mixer_out_kernel,
        out_shape=jax.ShapeDtypeStruct(x.shape, F32),
        grid=(b, t // tm),
        in_specs=[pl.BlockSpec((1, tm, d), tok),
                  pl.BlockSpec((1, tm, HALF_WIDTH), tok),
                  pl.BlockSpec((1, tm, HALF_WIDTH), tok),
                  pl.BlockSpec((1, N_MOD, d), lambda bi, ti: (bi, 0, 0)),
                  _resident(w_fox.shape),
                  _resident(w_moba.shape)],
        out_specs=pl.BlockSpec((1, tm, d), tok),
        compiler_params=_params(32, 2),
        name="mixer_out",
    )(x, o_fox, o_moba, mod_l, w_fox, w_moba)


def _d_term_routing():
    src = jnp.arange(LANES)[:, None]
    dst = jnp.arange(HALF_WIDTH)[None, :]
    mats = []
    for k in range(D_TERMS):
        want = (src // 2) * LANES + (src % 2) * D_TERMS + k
        mats.append(((dst == want) & (src < FOX_HEADS)).astype(BF16))
    return jnp.concatenate(mats, axis=0)


def kernel(x, c, positions, w_ada, b_ada, norm_g, ffn_w_in, ffn_w_out, w_mix_in, fox_b_f,
           qk_norm_g, w_mix_out):
    b, t, d = x.shape
    depth = w_ada.shape[0]
    assert d == D_MODEL and t % TOKEN_TILE == 0 and TOKEN_TILE % MOBA_BLOCK == 0
    assert TOKEN_TILE % ATT_TILE == 0 and 2 * D_TERMS <= AUG_ROWS
    hw = HALF_WIDTH

    mod = _adaln(c, w_ada, b_ada)
    cos_t, sin_t = _rope_tables(positions)

    w_in = ffn_w_in.astype(BF16)
    w_out = ffn_w_out.astype(BF16)
    f_lo = 3 * hw
    m_lo = f_lo + FOX_HEADS
    w_gate = jnp.pad(w_mix_in[:, :, f_lo:m_lo], ((0, 0), (0, 0), (0, LANES - FOX_HEADS)))
    w_mix = jnp.concatenate([w_mix_in[:, :, :f_lo], w_mix_in[:, :, m_lo:], w_gate],
                            axis=-1).astype(BF16)
    w_o = w_mix_out.astype(BF16)
    bf = jnp.pad(fox_b_f, ((0, 0), (0, LANES - FOX_HEADS))).reshape(depth, 1, LANES)
    gains = jnp.tile(qk_norm_g, (1, 1, FOX_HEADS))
    gains = gains * jnp.array([QK_SCALE * LOG2E, 1.0, 1.0, 1.0], F32)[None, :, None]
    heads_per_tile = MXU_DIM // HEAD_DIM
    gmat = jnp.kron(jnp.eye(heads_per_tile, dtype=F32),
                    jnp.full((HEAD_DIM, HEAD_DIM), 1.0 / HEAD_DIM, F32)).astype(BF16)
    sel = _d_term_routing()

    for l in range(depth):
        x = _ffn(x, mod[l], norm_g[l, 0], w_in[l, 0], w_out[l, 0], k0=0)
        fqt, fk, fvt, mqt, mqft, mk, mvt, km = _mixer_in(
            x, mod[l], norm_g[l, 1], w_mix[l], gmat, bf[l], gains[l], cos_t, sin_t, sel)
        o_fox = _attention(fqt, fk, fvt)
        o_moba = _attention(mqt, mk, mvt, qft=mqft, km=km)
        x = _mixer_out(x, o_fox, o_moba, mod[l], w_o[l, :hw], w_o[l, hw:])
        x = _ffn(x, mod[l], norm_g[l, 2], w_in[l, 1], w_out[l, 1], k0=6)
    return x
```

```python
import functools
import math

import jax
import jax.numpy as jnp
from jax import lax
from jax.experimental import pallas as pl
from jax.experimental.pallas import tpu as pltpu

F32 = jnp.float32
BF16 = jnp.bfloat16

D_MODEL = 1024
HEAD_DIM = 64
FOX_HEADS = 8
MOBA_HEADS = 8
HALF_WIDTH = FOX_HEADS * HEAD_DIM
D_FF = 2816
ROPE_THETA = 10000.0
MOBA_BLOCK = 256
MOBA_TOPK = 3
NORM_EPS = 1e-6
N_MOD = 9
NEG_INF = -1e30
LOG2E = math.log2(math.e)
QK_SCALE = HEAD_DIM ** -0.5

LANES = 128
HEAD_PAIRS = HALF_WIDTH // LANES
MXU_DIM = 256
K_AUG = MXU_DIM
AUG_ROWS = 16
D_TERMS = 3
ATT_TILE = 512
PAIRS_PER_STEP = 4
TOKEN_TILE = 512
MIB = 1024 * 1024


def _params(vmem_mib, n_axes):
    return pltpu.CompilerParams(dimension_semantics=("arbitrary",) * n_axes,
                                vmem_limit_bytes=vmem_mib * MIB)


def _resident(shape):
    return pl.BlockSpec(shape, lambda *_: (0,) * len(shape), pipeline_mode=pl.Buffered(1))


def _layer_block(shape, lead):
    rest = tuple(shape[len(lead):])
    return pl.BlockSpec((None,) * len(lead) + rest, lambda *_: tuple(lead) + (0,) * len(rest),
                        pipeline_mode=pl.Buffered(1))


def _norm_mod(x, g, shift, scale):
    ms = jnp.mean(x * x, axis=-1, keepdims=True)
    return (x * lax.rsqrt(ms + NORM_EPS) * g) * (1.0 + scale) + shift


def _adaln_kernel(c_ref, w_ref, b_ref, o_ref):
    c = c_ref[...]
    act = c * (1.0 / (1.0 + jnp.exp(-c)))
    o_ref[0] = jnp.dot(act, w_ref[0], preferred_element_type=F32,
                       precision=lax.Precision.HIGHEST) + b_ref[0]


def _adaln(c, w_ada, b_ada):
    depth, d, width = w_ada.shape
    b = c.shape[0]
    rows = 8
    c_pad = jnp.pad(c, ((0, rows - b), (0, 0)))
    out = pl.pallas_call(
        _adaln_kernel,
        out_shape=jax.ShapeDtypeStruct((depth, rows, width), F32),
        grid=(depth, width // d),
        in_specs=[pl.BlockSpec((rows, d), lambda l, n: (0, 0)),
                  pl.BlockSpec((1, d, d), lambda l, n: (l, 0, n)),
                  pl.BlockSpec((1, 1, d), lambda l, n: (l, 0, n))],
        out_specs=pl.BlockSpec((1, rows, d), lambda l, n: (l, 0, n)),
        compiler_params=_params(32, 2),
        name="adaln_mod",
    )(c_pad, w_ada, b_ada.reshape(depth, 1, width))
    return out[:, :b].reshape(depth, b, N_MOD, d)


def _rope_kernel(pos_ref, freq_ref, cos_ref, sin_ref):
    ang = pos_ref[...].astype(F32) * freq_ref[...]
    lane = lax.broadcasted_iota(jnp.int32, ang.shape, 1)
    sin = jnp.sin(ang)
    cos_ref[...] = jnp.cos(ang)
    sin_ref[...] = jnp.where((lane & (HEAD_DIM - 1)) < HEAD_DIM // 2, -sin, sin)


def _rope_tables(positions):
    b, t = positions.shape
    n = b * t
    tm = min(TOKEN_TILE, n)
    inv_freq = ROPE_THETA ** (-jnp.arange(0, HEAD_DIM, 2, dtype=F32) / HEAD_DIM)
    freq = jnp.tile(inv_freq, LANES // (HEAD_DIM // 2)).reshape(1, LANES)
    cos, sin = pl.pallas_call(
        _rope_kernel,
        out_shape=(jax.ShapeDtypeStruct((n, LANES), F32),) * 2,
        grid=(n // tm,),
        in_specs=[pl.BlockSpec((tm, 1), lambda i: (i, 0)),
                  pl.BlockSpec((1, LANES), lambda i: (0, 0))],
        out_specs=(pl.BlockSpec((tm, LANES), lambda i: (i, 0)),) * 2,
        compiler_params=_params(32, 1),
        name="rope_tables",
    )(positions.reshape(n, 1), freq)
    return cos.reshape(b, t, LANES), sin.reshape(b, t, LANES)


def _ffn_kernel(*refs, k0, mixer_out):
    if mixer_out:
        x_ref, mod_ref, g_ref, win_ref, wout_ref, of_ref, om_ref, wf_ref, wm_ref, o_ref = refs
    else:
        x_ref, mod_ref, g_ref, win_ref, wout_ref, o_ref = refs
    x = x_ref[0]
    mod = mod_ref[...]
    if mixer_out:
        y = (jnp.dot(of_ref[0], wf_ref[...], preferred_element_type=F32)
             + jnp.dot(om_ref[0], wm_ref[...], preferred_element_type=F32))
        x = x + mod[5:6] * y
    hb = _norm_mod(x, g_ref[...], mod[k0:k0 + 1], mod[k0 + 1:k0 + 2]).astype(BF16)
    gate = jnp.dot(hb, win_ref[:, :D_FF], preferred_element_type=F32)
    up = jnp.dot(hb, win_ref[:, D_FF:], preferred_element_type=F32)
    act = (gate * (1.0 / (1.0 + jnp.exp(-gate))) * up).astype(BF16)
    y = jnp.dot(act, wout_ref[...], preferred_element_type=F32)
    o_ref[0] = x + (0.5 * mod[k0 + 2:k0 + 3]) * y


def _ffn(x, mod, norm_g, w_in, w_out, layer, which, mixer=None):
    b, t, d = x.shape
    tm = min(TOKEN_TILE, t)
    tok = lambda bi, ti: (bi, ti, 0)
    in_specs = [pl.BlockSpec((1, tm, d), tok),
                pl.BlockSpec((None, None, N_MOD, d), lambda bi, ti: (layer, bi, 0, 0)),
                _layer_block(norm_g.shape, (layer, 2 * which)),
                _layer_block(w_in.shape, (layer, which)),
                _layer_block(w_out.shape, (layer, which))]
    args = [x, mod, norm_g, w_in, w_out]
    if mixer is not None:
        o_fox, o_moba, w_o = mixer
        in_specs += [pl.BlockSpec((1, tm, HALF_WIDTH), tok),
                     pl.BlockSpec((1, tm, HALF_WIDTH), tok),
                     _layer_block(w_o.shape, (layer, 0)),
                     _layer_block(w_o.shape, (layer, 1))]
        args += [o_fox, o_moba, w_o, w_o]
    return pl.pallas_call(
        functools.partial(_ffn_kernel, k0=3 * 2 * which, mixer_out=mixer is not None),
        out_shape=jax.ShapeDtypeStruct(x.shape, F32),
        grid=(b, t // tm),
        in_specs=in_specs,
        out_specs=pl.BlockSpec((1, tm, d), tok),
        compiler_params=_params(56, 2),
        name="mixer_out_ffn" if mixer is not None else "swiglu_ffn",
    )(*args)


def _mixer_in_kernel(x_ref, mod_ref, g_ref, w_ref, gmat_ref, bf_ref, qkg_ref, cos_ref, sin_ref,
                     sel_ref,
                     fqt_ref, fk_ref, fvt_ref, mqt_ref, mqft_ref, mk_ref, mvt_ref, km_ref,
                     carry_ref, *, tm):
    ti = pl.program_id(1)
    x = x_ref[0]
    mod = mod_ref[...]
    hb = _norm_mod(x, g_ref[...], mod[3:4], mod[4:5]).astype(BF16)
    gmat = gmat_ref[...]
    qkg = qkg_ref[...]

    def proj(k):
        return jnp.dot(hb, w_ref[:, k * HALF_WIDTH:(k + 1) * HALF_WIDTH],
                       preferred_element_type=F32)

    def head_norm(tt, gain):
        sq = (tt * tt).astype(BF16)
        ms = jnp.concatenate(
            [jnp.dot(sq[:, k * MXU_DIM:(k + 1) * MXU_DIM], gmat, preferred_element_type=F32)
             for k in range(HALF_WIDTH // MXU_DIM)], axis=1)
        return tt * lax.rsqrt(ms + NORM_EPS) * gain

    lane = lax.broadcasted_iota(jnp.int32, (tm, LANES), 1)
    row = lax.broadcasted_iota(jnp.int32, (tm, LANES), 0)
    first_half = (lane & (HEAD_DIM - 1)) < HEAD_DIM // 2
    cosb = cos_ref[0]
    sinb = sin_ref[0]

    def rope(tt):
        outs = []
        for cb in range(HEAD_PAIRS):
            tc = tt[:, cb * LANES:(cb + 1) * LANES]
            partner = jnp.where(first_half,
                                pltpu.roll(tc, LANES - HEAD_DIM // 2, axis=1),
                                pltpu.roll(tc, HEAD_DIM // 2, axis=1))
            outs.append(tc * cosb + partner * sinb)
        return jnp.concatenate(outs, axis=1)

    def store_keys(k_ref, keys, aug):
        kb = keys.astype(BF16)
        for p in range(HEAD_PAIRS):
            k_ref[0, p, :, 0:LANES] = kb[:, p * LANES:(p + 1) * LANES]
            k_ref[0, p, :, LANES:K_AUG] = aug[p]

    def store_values_t(vt_ref, vals):
        vt = vals.T.astype(BF16)
        for p in range(HEAD_PAIRS):
            for r in range(tm // ATT_TILE):
                vt_ref[0, p, r] = vt[p * LANES:(p + 1) * LANES, r * ATT_TILE:(r + 1) * ATT_TILE]

    z = jnp.dot(hb, w_ref[:, 6 * HALF_WIDTH:6 * HALF_WIDTH + LANES],
                preferred_element_type=F32) + bf_ref[...]
    cs = jnp.minimum(z, 0.0) - jnp.log1p(jnp.exp(-jnp.abs(z)))
    step = 1
    while step < tm:
        cs = cs + jnp.where(row >= step, pltpu.roll(cs, step, axis=0), 0.0)
        step *= 2

    @pl.when(ti == 0)
    def _():
        carry_ref[...] = jnp.zeros_like(carry_ref)

    cs = cs + carry_ref[...]
    carry_ref[...] = cs[tm - 1:tm, :]
    rest = cs * (-LOG2E)
    terms = []
    for _ in range(D_TERMS):
        terms.append(rest.astype(BF16))
        rest = rest - terms[-1].astype(F32)
    aug_fb = jnp.dot(jnp.concatenate(terms, axis=1), sel_ref[...],
                     preferred_element_type=F32).astype(BF16)

    fq = head_norm(proj(0), qkg[0:1])
    fqt_ref[0] = fq.T.astype(BF16)
    store_keys(fk_ref, head_norm(proj(1), qkg[1:2]),
               [aug_fb[:, p * LANES:(p + 1) * LANES] for p in range(HEAD_PAIRS)])
    store_values_t(fvt_ref, proj(2))

    mq_t = rope(head_norm(proj(3), qkg[2:3])).T
    mqft_ref[0] = mq_t
    mqt_ref[0] = (mq_t * (QK_SCALE * LOG2E)).astype(BF16)
    mk = rope(head_norm(proj(4), qkg[3:4]))
    block_onehot = jnp.where(lane == ((ti * tm + row) // MOBA_BLOCK), 1.0, 0.0).astype(BF16)
    store_keys(mk_ref, mk, [block_onehot] * HEAD_PAIRS)
    store_values_t(mvt_ref, proj(5))
    for r in range(tm // MOBA_BLOCK):
        km_ref[0, 0, r:r + 1, :] = jnp.mean(mk[r * MOBA_BLOCK:(r + 1) * MOBA_BLOCK],
                                            axis=0, keepdims=True)


def _mixer_in(x, mod, norm_g, w, gmat, bf, qkg, cos_t, sin_t, sel, layer):
    b, t, d = x.shape
    tm = min(TOKEN_TILE, t)
    nt = t // tm
    hw = HALF_WIDTH
    tok = lambda bi, ti: (bi, ti, 0)
    qt_spec = pl.BlockSpec((1, hw, tm), lambda bi, ti: (bi, 0, ti))
    k_spec = pl.BlockSpec((1, HEAD_PAIRS, tm, K_AUG), lambda bi, ti: (bi, 0, ti, 0))
    vt_spec = pl.BlockSpec((1, HEAD_PAIRS, tm // ATT_TILE, LANES, ATT_TILE),
                           lambda bi, ti: (bi, 0, ti, 0, 0))
    qt_shape = jax.ShapeDtypeStruct((b, hw, t), BF16)
    k_shape = jax.ShapeDtypeStruct((b, HEAD_PAIRS, t, K_AUG), BF16)
    vt_shape = jax.ShapeDtypeStruct((b, HEAD_PAIRS, t // ATT_TILE, LANES, ATT_TILE), BF16)
    outs = pl.pallas_call(
        functools.partial(_mixer_in_kernel, tm=tm),
        out_shape=(qt_shape, k_shape, vt_shape,
                   qt_shape, jax.ShapeDtypeStruct((b, hw, t), F32), k_shape, vt_shape,
                   jax.ShapeDtypeStruct((b, nt, tm // MOBA_BLOCK, hw), F32)),
        grid=(b, nt),
        in_specs=[pl.BlockSpec((1, tm, d), tok),
                  pl.BlockSpec((None, None, N_MOD, d), lambda bi, ti: (layer, bi, 0, 0)),
                  _layer_block(norm_g.shape, (layer, 1)),
                  _layer_block(w.shape, (layer,)),
                  _resident(gmat.shape),
                  _layer_block(bf.shape, (layer,)),
                  _layer_block(qkg.shape, (layer,)),
                  pl.BlockSpec((1, tm, LANES), tok),
                  pl.BlockSpec((1, tm, LANES), tok),
                  _resident(sel.shape)],
        out_specs=(qt_spec, k_spec, vt_spec, qt_spec, qt_spec, k_spec, vt_spec,
                   pl.BlockSpec((1, 1, tm // MOBA_BLOCK, hw), lambda bi, ti: (bi, ti, 0, 0))),
        scratch_shapes=[pltpu.VMEM((1, LANES), F32)],
        compiler_params=_params(56, 2),
        name="mixer_in",
    )(x, mod, norm_g, w, gmat, bf, qkg, cos_t, sin_t, sel)
    fqt, fk, fvt, mqt, mqft, mk, mvt, km = outs
    return fqt, fk, fvt, mqt, mqft, mk, mvt, km.reshape(b, t // MOBA_BLOCK, hw)


def _moba_bias(qft, km, i, tq, nblk):
    gate = jnp.dot(km, qft, preferred_element_type=F32,
                   precision=lax.Precision.HIGHEST)
    blk = lax.broadcasted_iota(jnp.int32, (nblk, tq), 0)
    qblk = (i * tq + lax.broadcasted_iota(jnp.int32, (1, tq), 1)) // MOBA_BLOCK
    rank = jnp.zeros((nblk, tq), jnp.int32)
    for jp in range(nblk):
        gj = gate[jp:jp + 1, :]
        beats = (gj > gate) | ((gj == gate) & (jp < blk))
        rank = rank + jnp.where(beats & (jp < qblk), 1, 0)
    keep = ((blk < qblk) & (rank < MOBA_TOPK)) | (blk == qblk)
    bias = jnp.where(keep, 0.0, NEG_INF)
    if nblk < AUG_ROWS:
        bias = jnp.concatenate([bias, jnp.full((AUG_ROWS - nblk, tq), NEG_INF, F32)], axis=0)
    return bias


def _attn_kernel(*refs, tq, npairs, nblk, moba):
    if moba:
        qt_ref, qft_ref, km_ref, k_ref, vt_ref, o_ref, qaug_ref, acc_ref, m_ref, l_ref = refs
    else:
        qt_ref, k_ref, vt_ref, o_ref, qaug_ref, acc_ref, m_ref, l_ref = refs
    i = pl.program_id(2)
    nheads = 2 * npairs
    half = jnp.zeros((HEAD_DIM, tq), BF16)
    tail = jnp.zeros((K_AUG - LANES - AUG_ROWS, tq), BF16)
    aug_row = lax.broadcasted_iota(jnp.int32, (AUG_ROWS, tq), 0)

    for pp in range(npairs):
        qt = qt_ref[0, pp * LANES:(pp + 1) * LANES, :]
        for h in range(2):
            if moba:
                qft = qft_ref[0, pp * LANES:(pp + 1) * LANES, :]
                zf = jnp.zeros((HEAD_DIM, tq), F32)
                qft_h = (jnp.concatenate([qft[:HEAD_DIM], zf], axis=0) if h == 0
                         else jnp.concatenate([zf, qft[HEAD_DIM:]], axis=0))
                aug = _moba_bias(qft_h, km_ref[0, :, pp * LANES:(pp + 1) * LANES], i, tq, nblk)
            else:
                aug = jnp.where((aug_row >= D_TERMS * h) & (aug_row < D_TERMS * (h + 1)), 1.0, 0.0)
            top = (jnp.concatenate([qt[:HEAD_DIM], half], axis=0) if h == 0
                   else jnp.concatenate([half, qt[HEAD_DIM:]], axis=0))
            qaug_ref[2 * pp + h] = jnp.concatenate([top, aug.astype(BF16), tail], axis=0)

    m_ref[...] = jnp.full(m_ref.shape, NEG_INF, F32)
    l_ref[...] = jnp.zeros(l_ref.shape, F32)
    acc_ref[...] = jnp.zeros(acc_ref.shape, F32)
    visible = (lax.broadcasted_iota(jnp.int32, (tq, tq), 0)
               <= lax.broadcasted_iota(jnp.int32, (tq, tq), 1))

    ones = jnp.ones((AUG_ROWS, tq), BF16)

    def step(j, diag):
        start = pl.multiple_of(j * tq, tq)
        scores = []
        for pp in range(npairs):
            kj = k_ref[0, pp, pl.ds(start, tq), :]
            for h in range(2):
                s = jnp.dot(kj, qaug_ref[2 * pp + h], preferred_element_type=F32)
                if diag:
                    s = jnp.where(visible, s, NEG_INF)
                scores.append(s)
        for pp in range(npairs):
            vtj = jnp.concatenate([vt_ref[0, pp, j], ones], axis=0)
            for h in range(2):
                hh = 2 * pp + h
                s = scores[hh]
                m_old = m_ref[hh]
                m_new = jnp.maximum(m_old, jnp.max(s, axis=0, keepdims=True))
                p = jnp.exp2(s - m_new).astype(BF16)
                alpha = jnp.exp2(m_old - m_new)
                m_ref[hh] = m_new
                pv = jnp.dot(vtj, p, preferred_element_type=F32)
                lo, hi = h * HEAD_DIM, (h + 1) * HEAD_DIM
                acc_ref[pp, lo:hi, :] = alpha * acc_ref[pp, lo:hi, :] + pv[lo:hi, :]
                l_ref[hh] = alpha * l_ref[hh] + pv[LANES:LANES + 1, :]

    step(i, True)

    def body(j, carry):
        step(j, False)
        return carry

    lax.fori_loop(0, i, body, 0)

    for pp in range(npairs):
        denom = jnp.concatenate([jnp.broadcast_to(l_ref[2 * pp + h], (HEAD_DIM, tq))
                                 for h in range(2)], axis=0)
        o_ref[0, :, pp * LANES:(pp + 1) * LANES] = (acc_ref[pp] / denom).T.astype(BF16)


def _attention(qt, k_aug, vt, qft=None, km=None):
    b, hw, t = qt.shape
    tq = min(ATT_TILE, t)
    npairs = PAIRS_PER_STEP
    width = npairs * LANES
    nblk = t // MOBA_BLOCK
    moba = qft is not None
    assert t % tq == 0 and nblk <= AUG_ROWS and vt.shape[-1] == tq
    q_spec = pl.BlockSpec((1, width, tq), lambda bi, g, i: (bi, g, i))
    in_specs = [q_spec]
    args = [qt]
    if moba:
        in_specs += [q_spec, pl.BlockSpec((1, nblk, width), lambda bi, g, i: (bi, 0, g))]
        args += [qft, km]
    in_specs += [pl.BlockSpec((1, npairs, t, K_AUG), lambda bi, g, i: (bi, g, 0, 0)),
                 pl.BlockSpec((1, npairs, t // tq, LANES, tq), lambda bi, g, i: (bi, g, 0, 0, 0))]
    args += [k_aug, vt]
    return pl.pallas_call(
        functools.partial(_attn_kernel, tq=tq, npairs=npairs, nblk=nblk, moba=moba),
        out_shape=jax.ShapeDtypeStruct((b, t, hw), BF16),
        grid=(b, HEAD_PAIRS // npairs, t // tq),
        in_specs=in_specs,
        out_specs=pl.BlockSpec((1, tq, width), lambda bi, g, i: (bi, i, g)),
        scratch_shapes=[pltpu.VMEM((2 * npairs, K_AUG, tq), BF16),
                        pltpu.VMEM((npairs, LANES, tq), F32),
                        pltpu.VMEM((2 * npairs, 1, tq), F32),
                        pltpu.VMEM((2 * npairs, 1, tq), F32)],
        compiler_params=_params(48, 3),
        name="moba_attention" if moba else "fox_attention",
    )(*args)


def _d_term_routing():
    src = jnp.arange(LANES)[:, None]
    dst = jnp.arange(HALF_WIDTH)[None, :]
    mats = []
    for k in range(D_TERMS):
        want = (src // 2) * LANES + (src % 2) * D_TERMS + k
        mats.append(((dst == want) & (src < FOX_HEADS)).astype(BF16))
    return jnp.concatenate(mats, axis=0)


def kernel(x, c, positions, w_ada, b_ada, norm_g, ffn_w_in, ffn_w_out, w_mix_in, fox_b_f,
           qk_norm_g, w_mix_out):
    b, t, d = x.shape
    depth = w_ada.shape[0]
    assert d == D_MODEL and t % TOKEN_TILE == 0 and TOKEN_TILE % MOBA_BLOCK == 0
    assert TOKEN_TILE % ATT_TILE == 0 and 2 * D_TERMS <= AUG_ROWS
    hw = HALF_WIDTH

    mod = _adaln(c, w_ada, b_ada)
    cos_t, sin_t = _rope_tables(positions)

    w_in = ffn_w_in.astype(BF16)
    w_out = ffn_w_out.astype(BF16)
    f_lo = 3 * hw
    m_lo = f_lo + FOX_HEADS
    w_gate = jnp.pad(w_mix_in[:, :, f_lo:m_lo], ((0, 0), (0, 0), (0, LANES - FOX_HEADS)))
    w_mix = jnp.concatenate([w_mix_in[:, :, :f_lo], w_mix_in[:, :, m_lo:], w_gate],
                            axis=-1).astype(BF16)
    w_o = w_mix_out.astype(BF16).reshape(depth, 2, hw, d)
    norm_g = norm_g.reshape(depth, 3, 1, d)
    bf = jnp.pad(fox_b_f, ((0, 0), (0, LANES - FOX_HEADS))).reshape(depth, 1, LANES)
    gains = jnp.tile(qk_norm_g, (1, 1, FOX_HEADS))
    gains = gains * jnp.array([QK_SCALE * LOG2E, 1.0, 1.0, 1.0], F32)[None, :, None]
    heads_per_tile = MXU_DIM // HEAD_DIM
    gmat = jnp.kron(jnp.eye(heads_per_tile, dtype=F32),
                    jnp.full((HEAD_DIM, HEAD_DIM), 1.0 / HEAD_DIM, F32)).astype(BF16)
    sel = _d_term_routing()

    for l in range(depth):
        x = _ffn(x, mod, norm_g, w_in, w_out, l, 0)
        fqt, fk, fvt, mqt, mqft, mk, mvt, km = _mixer_in(
            x, mod, norm_g, w_mix, gmat, bf, gains, cos_t, sin_t, sel, l)
        o_fox = _attention(fqt, fk, fvt)
        o_moba = _attention(mqt, mk, mvt, qft=mqft, km=km)
        x = _ffn(x, mod, norm_g, w_in, w_out, l, 1, mixer=(o_fox, o_moba, w_o))
    return x
```
